```python
import jax
import jax.numpy as jnp
from jax import lax
import numpy as np

D_MODEL = 4096
BATCH = 4
SEQ = 2048
DEPTH = 4
DEC_BATCH = 8
DEC_SEQ = 1
PAST_LEN = 8192
PAGE_SIZE = 128

HEAD_DIM = 128
N_HEADS_A = D_MODEL // (2 * HEAD_DIM)
WIDTH_A = N_HEADS_A * HEAD_DIM
WIDTH_B = D_MODEL - WIDTH_A
CONV_W = 3
POOL_WINDOWS = (2, 4, 8, 16)
N_POOL_GROUPS = len(POOL_WINDOWS)
POOL_GROUP = D_MODEL // N_POOL_GROUPS
POOL_HIST = max(POOL_WINDOWS) - 1
D_FF = 11008
FFN_CONV_W = 3
Q_BLOCK = 128
N_ATTN = (DEPTH + 1) // 2
N_POOLMIX = DEPTH // 2
RMS_EPS = 1e-6
FORGET_BIAS_INIT = 3.0
IN_WIDTH = 3 * WIDTH_A + N_HEADS_A + 3 * WIDTH_B
IN_SPLITS = (WIDTH_A, 2 * WIDTH_A, 3 * WIDTH_A, 3 * WIDTH_A + N_HEADS_A,
             3 * WIDTH_A + N_HEADS_A + WIDTH_B, 3 * WIDTH_A + N_HEADS_A + 2 * WIDTH_B)

kernel_name = 'hybrid_fox_shortconv_pool_step'


def rmsnorm(x, g):
    xf = x.astype(jnp.float32)
    y = xf * lax.rsqrt(jnp.mean(xf * xf, axis=-1, keepdims=True) + RMS_EPS)
    return (y * g.astype(jnp.float32)).astype(x.dtype)


def causal_dwconv(hist, x, w):
    width, T = w.shape[0], x.shape[1]
    ctx = jnp.concatenate([hist, x], axis=1)
    y = ctx[:, 0:T] * w[0]
    for i in range(1, width):
        y = y + ctx[:, i:i + T] * w[i]
    return y, ctx[:, T:]


def even_in_proj(h, w_in, b_f):
    B, T, _ = h.shape
    z = h @ w_in
    q, k, v, f, gate_b, gate_c, hv = jnp.split(z, IN_SPLITS, axis=-1)
    shp = (B, T, N_HEADS_A, HEAD_DIM)
    logf = jax.nn.log_sigmoid(f.astype(jnp.float32) + b_f.astype(jnp.float32))
    return q.reshape(shp), k.reshape(shp), v.reshape(shp), logf, gate_b, gate_c * hv


def even_out(att, conv_gated, w_out):
    B, T = att.shape[:2]
    return jnp.concatenate([att.reshape(B, T, WIDTH_A), conv_gated], axis=-1) @ w_out


def fox_prompt(q, k, v, logf):
    B, S, H, Dh = q.shape
    nb = S // Q_BLOCK
    scale = Dh ** -0.5
    c = jnp.cumsum(logf, axis=1)
    ck = c.transpose(0, 2, 1)[:, :, None, :]
    qb = q.reshape(B, nb, Q_BLOCK, H, Dh).transpose(1, 0, 2, 3, 4)
    cb = c.reshape(B, nb, Q_BLOCK, H).transpose(1, 0, 2, 3)
    kpos = jnp.arange(S)

    def block(args):
        qi, ci, bi = args
        s = jnp.einsum('bqhd,bkhd->bhqk', qi, k, preferred_element_type=jnp.float32) * scale
        s = s + ci.transpose(0, 2, 1)[..., None] - ck
        qpos = bi * Q_BLOCK + jnp.arange(Q_BLOCK)
        s = jnp.where(qpos[:, None] >= kpos[None, :], s, -jnp.inf)
        p = jax.nn.softmax(s, axis=-1)
        return jnp.einsum('bhqk,bkhd->bqhd', p.astype(v.dtype), v)

    o = lax.map(block, (qb, cb, jnp.arange(nb)))
    return o.transpose(1, 0, 2, 3, 4).reshape(B, S, H, Dh)


def fox_sample(q, k, v, logf, k_past, v_past, logf_past):
    P, T = k_past.shape[1], q.shape[1]
    scale = HEAD_DIM ** -0.5
    c = jnp.cumsum(jnp.concatenate([logf_past.astype(jnp.float32), logf], axis=1), axis=1)
    s = jnp.concatenate([
        jnp.einsum('bqhd,bkhd->bhqk', q, k_past, preferred_element_type=jnp.float32),
        jnp.einsum('bqhd,bkhd->bhqk', q, k, preferred_element_type=jnp.float32)], axis=-1) * scale
    s = s + c[:, P:].transpose(0, 2, 1)[..., None] - c.transpose(0, 2, 1)[:, :, None, :]
    mask = jnp.concatenate([jnp.ones((T, P), bool), jnp.tril(jnp.ones((T, T), bool))], axis=1)
    p = jax.nn.softmax(jnp.where(mask, s, -jnp.inf), axis=-1)
    return (jnp.einsum('bhqk,bkhd->bqhd', p[..., :P].astype(v.dtype), v_past)
            + jnp.einsum('bhqk,bkhd->bqhd', p[..., P:].astype(v.dtype), v))


def pool_mix(hist, x, start, w_g, scale):
    B, T, _ = x.shape
    ctx = jnp.concatenate([hist, x], axis=1)
    cf = ctx.astype(jnp.float32)
    cs = jnp.concatenate([jnp.zeros_like(cf[:, :1]), jnp.cumsum(cf, axis=1)], axis=1)
    pos = start + jnp.arange(T)
    groups = []
    for g, w in enumerate(POOL_WINDOWS):
        c0 = g * POOL_GROUP
        hi = cs[:, POOL_HIST + 1:POOL_HIST + 1 + T, c0:c0 + POOL_GROUP]
        lo = cs[:, POOL_HIST + 1 - w:POOL_HIST + 1 - w + T, c0:c0 + POOL_GROUP]
        cnt = jnp.minimum(pos + 1, w).astype(jnp.float32)[None, :, None]
        groups.append((hi - lo) / cnt)
    pooled = jnp.stack(groups, axis=2) - cf[:, POOL_HIST:].reshape(B, T, N_POOL_GROUPS, POOL_GROUP)
    mixed = jnp.einsum('btgc,gcd->btgd', pooled.astype(x.dtype), w_g).reshape(B, T, D_MODEL)
    return (mixed * scale).astype(x.dtype), ctx[:, T:]


def conv_ffn(h, hist, w_up_l, w_conv_l, b_conv_l, w_down_l):
    up = h @ w_up_l
    y, new_hist = causal_dwconv(hist, up, w_conv_l)
    gate, val = jnp.split(y + b_conv_l, 2, axis=-1)
    return (jax.nn.silu(gate) * val) @ w_down_l, new_hist


def _normal(k, shape, s):
    return s * jax.random.normal(k, shape, jnp.float32)


def setup_inputs(seed: int = 0) -> dict:
    key = jax.random.key(seed)
    ks = jax.random.split(key, 23)
    n_pages = PAST_LEN // PAGE_SIZE
    n_used = DEC_BATCH * n_pages
    n_pool = n_used + max(1, n_used // 4)
    page_table = jax.random.permutation(ks[5], n_pool)[:n_used].reshape(DEC_BATCH, n_pages).astype(jnp.int32)
    return {
        'x_prompt': _normal(ks[0], (BATCH, SEQ, D_MODEL), 1.0),
        'x_sample': _normal(ks[1], (DEC_BATCH, DEC_SEQ, D_MODEL), 1.0),
        'cache_k': _normal(ks[2], (N_ATTN, n_pool, PAGE_SIZE, N_HEADS_A, HEAD_DIM), 1.0),
        'cache_v': _normal(ks[3], (N_ATTN, n_pool, PAGE_SIZE, N_HEADS_A, HEAD_DIM), 1.0),
        'cache_logf': jax.nn.log_sigmoid(FORGET_BIAS_INIT + _normal(ks[4], (N_ATTN, n_pool, PAGE_SIZE, N_HEADS_A), 1.0)),
        'page_table': page_table,
        'state_conv_b': _normal(ks[6], (N_ATTN, DEC_BATCH, CONV_W - 1, WIDTH_B), 1.0),
        'state_pool': _normal(ks[7], (N_POOLMIX, DEC_BATCH, POOL_HIST, D_MODEL), 1.0),
        'state_ffn_conv': _normal(ks[8], (DEPTH, DEC_BATCH, FFN_CONV_W - 1, 2 * D_FF), 1.0),
        'w_in_mix': _normal(ks[9], (N_ATTN, D_MODEL, IN_WIDTH), D_MODEL ** -0.5),
        'b_forget': FORGET_BIAS_INIT + _normal(ks[10], (N_ATTN, N_HEADS_A), 0.1),
        'w_conv_b': _normal(ks[11], (N_ATTN, CONV_W, WIDTH_B), CONV_W ** -0.5),
        'w_out_mix': _normal(ks[12], (N_ATTN, D_MODEL, D_MODEL), D_MODEL ** -0.5),
        'w_pool': _normal(ks[13], (N_POOLMIX, N_POOL_GROUPS, POOL_GROUP, POOL_GROUP), POOL_GROUP ** -0.5),
        'pool_scale': 1.0 + _normal(ks[14], (N_POOLMIX, D_MODEL), 0.1),
        'w_up': _normal(ks[15], (DEPTH, D_MODEL, 2 * D_FF), D_MODEL ** -0.5),
        'w_ffn_conv': _normal(ks[16], (DEPTH, FFN_CONV_W, 2 * D_FF), FFN_CONV_W ** -0.5),
        'b_ffn_conv': _normal(ks[17], (DEPTH, 2 * D_FF), 0.01),
        'w_down': _normal(ks[18], (DEPTH, D_FF, D_MODEL), D_FF ** -0.5),
        'g_mix_pre': 1.0 + _normal(ks[19], (DEPTH, D_MODEL), 0.05),
        'g_mix_post': 1.0 + _normal(ks[20], (DEPTH, D_MODEL), 0.05),
        'g_ffn_pre': 1.0 + _normal(ks[21], (DEPTH, D_MODEL), 0.05),
        'g_ffn_post': 1.0 + _normal(ks[22], (DEPTH, D_MODEL), 0.05),
    }


def reference(x_prompt, x_sample, cache_k, cache_v, cache_logf, page_table,
              state_conv_b, state_pool, state_ffn_conv,
              w_in_mix, b_forget, w_conv_b, w_out_mix, w_pool, pool_scale,
              w_up, w_ffn_conv, b_ffn_conv, w_down,
              g_mix_pre, g_mix_post, g_ffn_pre, g_ffn_post):
    n_batch, n_dec = x_prompt.shape[0], x_sample.shape[0]
    past = page_table.shape[1] * cache_k.shape[2]
    xp, xs = x_prompt, x_sample
    kp_l, vp_l, lp_l, ks_l, vs_l, ls_l = [], [], [], [], [], []
    cbp_l, cbs_l, plp_l, pls_l, ffp_l, ffs_l = [], [], [], [], [], []
    for layer in range(DEPTH):
        j = layer // 2
        hp = rmsnorm(xp, g_mix_pre[layer])
        hs = rmsnorm(xs, g_mix_pre[layer])
        if layer % 2 == 0:
            qp, kp, vp, lfp, gbp, up_ = even_in_proj(hp, w_in_mix[j], b_forget[j])
            qs, ks, vs, lfs, gbs, us = even_in_proj(hs, w_in_mix[j], b_forget[j])
            k_past = cache_k[j][page_table].reshape(n_dec, past, N_HEADS_A, HEAD_DIM)
            v_past = cache_v[j][page_table].reshape(n_dec, past, N_HEADS_A, HEAD_DIM)
            lf_past = cache_logf[j][page_table].reshape(n_dec, past, N_HEADS_A)
            att_p = fox_prompt(qp, kp, vp, lfp)
            att_s = fox_sample(qs, ks, vs, lfs, k_past, v_past, lf_past)
            conv_p, hist_bp = causal_dwconv(jnp.zeros((n_batch, CONV_W - 1, WIDTH_B), up_.dtype), up_, w_conv_b[j])
            conv_s, hist_bs = causal_dwconv(state_conv_b[j], us, w_conv_b[j])
            mp = even_out(att_p, gbp * conv_p, w_out_mix[j])
            ms = even_out(att_s, gbs * conv_s, w_out_mix[j])
            kp_l.append(kp); vp_l.append(vp); lp_l.append(lfp)
            ks_l.append(ks); vs_l.append(vs); ls_l.append(lfs)
            cbp_l.append(hist_bp); cbs_l.append(hist_bs)
        else:
            mp, hist_pp = pool_mix(jnp.zeros((n_batch, POOL_HIST, D_MODEL), hp.dtype), hp, 0, w_pool[j], pool_scale[j])
            ms, hist_ps = pool_mix(state_pool[j], hs, past, w_pool[j], pool_scale[j])
            plp_l.append(hist_pp); pls_l.append(hist_ps)
        xp = xp + rmsnorm(mp, g_mix_post[layer])
        xs = xs + rmsnorm(ms, g_mix_post[layer])
        fp, hist_fp = conv_ffn(rmsnorm(xp, g_ffn_pre[layer]),
                               jnp.zeros((n_batch, FFN_CONV_W - 1, 2 * D_FF), xp.dtype),
                               w_up[layer], w_ffn_conv[layer], b_ffn_conv[layer], w_down[layer])
        fs, hist_fs = conv_ffn(rmsnorm(xs, g_ffn_pre[layer]), state_ffn_conv[layer],
                               w_up[layer], w_ffn_conv[layer], b_ffn_conv[layer], w_down[layer])
        xp = xp + rmsnorm(fp, g_ffn_post[layer])
        xs = xs + rmsnorm(fs, g_ffn_post[layer])
        ffp_l.append(hist_fp); ffs_l.append(hist_fs)
    return (xp, xs,
            jnp.stack(kp_l), jnp.stack(vp_l), jnp.stack(lp_l),
            jnp.stack(ks_l), jnp.stack(vs_l), jnp.stack(ls_l),
            jnp.stack(cbp_l), jnp.stack(cbs_l),
            jnp.stack(plp_l), jnp.stack(pls_l),
            jnp.stack(ffp_l), jnp.stack(ffs_l))
```

```python
import functools

import jax
import jax.numpy as jnp
from jax import lax
from jax.experimental import pallas as pl
from jax.experimental.pallas import tpu as pltpu

F32 = jnp.float32
BF16 = jnp.bfloat16

RMS_EPS = 1e-6
POOL_WINDOWS = (2, 4, 8, 16)
CONV_W = 3

V7X_LANES = 128
V7X_SUBLANES = 8
V7X_BF16_ROWS = 16
V7X_MXU_DIM = 256
V7X_VMEM_LIMIT_BYTES = 56 * 1024 * 1024

SAMPLE_ROWS = V7X_BF16_ROWS

PROMPT_ROW_TILE = 1024
ATTN_TILE = 512
POOL_ROW_TILE = 256


def _cparams(n_axes):
    return pltpu.CompilerParams(
        dimension_semantics=("arbitrary",) * n_axes,
        vmem_limit_bytes=V7X_VMEM_LIMIT_BYTES)


def _pick_tile(n, pref, align):
    t = min(pref, n)
    t -= t % align
    while t >= align:
        if n % t == 0:
            return t
        t -= align
    return n


def _round_up(x, m):
    return (x + m - 1) // m * m


def _rms(x, g):
    ms = jnp.mean(x * x, axis=-1, keepdims=True)
    return (x * lax.rsqrt(ms + RMS_EPS)) * g


def _cast_rows(src_ref, dst_ref, nrows, chunk, row0=None, valid_rows=None):
    def body(i, c):
        r = pl.multiple_of(i * chunk, chunk)
        w = src_ref[pl.ds(r, chunk), :]
        if valid_rows is not None:
            gr = row0 + r + lax.broadcasted_iota(jnp.int32, (chunk, 1), 0)
            w = jnp.where(gr < valid_rows, w, 0.0)
        dst_ref[pl.ds(r, chunk), :] = w.astype(BF16)
        return c
    lax.fori_loop(0, nrows // chunk, body, 0)


def _shift_rows(a, carry8, k):
    rows = a.shape[0]
    r = pltpu.roll(a, k, axis=0)
    row8 = lax.broadcasted_iota(jnp.int32, (V7X_SUBLANES, 1), 0)
    top = jnp.where(row8 < k, pltpu.roll(carry8, k, axis=0), r[0:V7X_SUBLANES])
    if rows == V7X_SUBLANES:
        return top
    return jnp.concatenate([top, r[V7X_SUBLANES:]], axis=0)


def _causal_conv3(u, carry8, w):
    y = _shift_rows(u, carry8, 2) * w[0:1]
    y = y + _shift_rows(u, carry8, 1) * w[1:2]
    return y + u * w[2:3]


def _silu(x):
    return x * (1.0 / (1.0 + jnp.exp(-x)))


def _rms_cast_body(x_ref, g_ref, o_ref):
    o_ref[...] = _rms(x_ref[...], g_ref[...]).astype(o_ref.dtype)


def _rms_cast(x, g):
    M, D = x.shape
    tm = _pick_tile(M, 256, V7X_BF16_ROWS)
    return pl.pallas_call(
        _rms_cast_body,
        grid=(M // tm,),
        in_specs=[pl.BlockSpec((tm, D), lambda m: (m, 0)),
                  pl.BlockSpec((1, D), lambda m: (0, 0))],
        out_specs=pl.BlockSpec((tm, D), lambda m: (m, 0)),
        out_shape=jax.ShapeDtypeStruct((M, D), BF16),
        compiler_params=_cparams(1), name="rms_cast")(x, g.reshape(1, D))


def _resnorm_body(x_ref, f_ref, gp_ref, gn_ref, xo_ref, ho_ref):
    xn = x_ref[...] + _rms(f_ref[...], gp_ref[...])
    xo_ref[...] = xn
    ho_ref[...] = _rms(xn, gn_ref[...]).astype(BF16)


def _resnorm_last_body(x_ref, f_ref, gp_ref, xo_ref):
    xo_ref[...] = x_ref[...] + _rms(f_ref[...], gp_ref[...])


def _resnorm(x, f, g_post, g_next):
    M, D = x.shape
    tm = _pick_tile(M, 256, V7X_BF16_ROWS)
    row = pl.BlockSpec((tm, D), lambda m: (m, 0))
    gain = pl.BlockSpec((1, D), lambda m: (0, 0))
    if g_next is None:
        xo = pl.pallas_call(
            _resnorm_last_body, grid=(M // tm,),
            in_specs=[row, row, gain], out_specs=row,
            out_shape=jax.ShapeDtypeStruct((M, D), F32),
            compiler_params=_cparams(1), name="resnorm_last")(x, f, g_post.reshape(1, D))
        return xo, None
    return pl.pallas_call(
        _resnorm_body, grid=(M // tm,),
        in_specs=[row, row, gain, gain], out_specs=[row, row],
        out_shape=[jax.ShapeDtypeStruct((M, D), F32), jax.ShapeDtypeStruct((M, D), BF16)],
        compiler_params=_cparams(1), name="resnorm")(
            x, f, g_post.reshape(1, D), g_next.reshape(1, D))


def _mm_body(h_ref, w_ref, o_ref, wb_ref, *, K, chunk):
    @pl.when(pl.program_id(1) == 0)
    def _():
        _cast_rows(w_ref, wb_ref, K, chunk)
    o_ref[...] = jnp.dot(h_ref[...], wb_ref[...],
                         preferred_element_type=F32).astype(o_ref.dtype)


def _mm(h, w, wsel, col_blk0, n_out, tn, tm, out_dtype, name):
    M, K = h.shape
    lead = (None,) * len(wsel)
    return pl.pallas_call(
        functools.partial(_mm_body, K=K, chunk=_pick_tile(K, 512, V7X_BF16_ROWS)),
        grid=(pl.cdiv(n_out, tn), M // tm),
        in_specs=[pl.BlockSpec((tm, K), lambda n, m: (m, 0)),
                  pl.BlockSpec(lead + (K, tn), lambda n, m: wsel + (0, n + col_blk0))],
        out_specs=pl.BlockSpec((tm, tn), lambda n, m: (m, n)),
        out_shape=jax.ShapeDtypeStruct((M, n_out), out_dtype),
        scratch_shapes=[pltpu.VMEM((K, tn), BF16)],
        compiler_params=_cparams(2), name=name)(h, w)


def _logf_body(h_ref, w_ref, b_ref, o_ref, wb_ref, *, K, chunk, H):
    @pl.when(pl.program_id(0) == 0)
    def _():
        _cast_rows(w_ref, wb_ref, K, chunk)
    z = jnp.dot(h_ref[...], wb_ref[...], preferred_element_type=F32)
    x = z[:, :H] + b_ref[...]
    o_ref[...] = jnp.minimum(x, 0.0) - jnp.log1p(jnp.exp(-jnp.abs(x)))


def _logf(h, w, wsel, col_blk, b, tm):
    M, K = h.shape
    H = b.shape[-1]
    lead = (None,) * len(wsel)
    return pl.pallas_call(
        functools.partial(_logf_body, K=K, chunk=_pick_tile(K, 512, V7X_BF16_ROWS), H=H),
        grid=(M // tm,),
        in_specs=[pl.BlockSpec((tm, K), lambda m: (m, 0)),
                  pl.BlockSpec(lead + (K, V7X_LANES), lambda m: wsel + (0, col_blk)),
                  pl.BlockSpec((1, H), lambda m: (0, 0))],
        out_specs=pl.BlockSpec((tm, H), lambda m: (m, 0)),
        out_shape=jax.ShapeDtypeStruct((M, H), F32),
        scratch_shapes=[pltpu.VMEM((K, V7X_LANES), BF16)],
        compiler_params=_cparams(1), name="logf")(h, w, b.reshape(1, H))


def _mm2_body(a_ref, b_ref, w_ref, o_ref, wb_ref, *, KA, K, chunk):
    @pl.when(pl.program_id(1) == 0)
    def _():
        _cast_rows(w_ref, wb_ref, K, chunk)
    o_ref[...] = (jnp.dot(a_ref[...], wb_ref[0:KA, :], preferred_element_type=F32)
                  + jnp.dot(b_ref[...], wb_ref[KA:K, :], preferred_element_type=F32))


def _mm2(a, b, w, wsel, tn, tm, name):
    M, KA = a.shape
    KB = b.shape[1]
    K = KA + KB
    N = w.shape[-1]
    lead = (None,) * len(wsel)
    return pl.pallas_call(
        functools.partial(_mm2_body, KA=KA, K=K, chunk=_pick_tile(K, 512, V7X_BF16_ROWS)),
        grid=(N // tn, M // tm),
        in_specs=[pl.BlockSpec((tm, KA), lambda n, m: (m, 0)),
                  pl.BlockSpec((tm, KB), lambda n, m: (m, 0)),
                  pl.BlockSpec(lead + (K, tn), lambda n, m: wsel + (0, n))],
        out_specs=pl.BlockSpec((tm, tn), lambda n, m: (m, n)),
        out_shape=jax.ShapeDtypeStruct((M, N), F32),
        scratch_shapes=[pltpu.VMEM((K, tn), BF16)],
        compiler_params=_cparams(2), name=name)(a, b, w)


def _convb_body(h_ref, wgb_ref, wgc_ref, whv_ref, cw_ref, o_ref, hist_ref,
                wb_ref, carry_ref, *, K, chunk, tm, tps):
    m = pl.program_id(1)

    @pl.when(m == 0)
    def _():
        _cast_rows(wgb_ref, wb_ref.at[0], K, chunk)
        _cast_rows(wgc_ref, wb_ref.at[1], K, chunk)
        _cast_rows(whv_ref, wb_ref.at[2], K, chunk)

    @pl.when(m % tps == 0)
    def _():
        carry_ref[...] = jnp.zeros_like(carry_ref)

    h = h_ref[...]
    gb = jnp.dot(h, wb_ref[0], preferred_element_type=F32)
    gc = jnp.dot(h, wb_ref[1], preferred_element_type=F32)
    hv = jnp.dot(h, wb_ref[2], preferred_element_type=F32)
    u = gc * hv
    y = _causal_conv3(u, carry_ref[...], cw_ref[...])
    o_ref[...] = (gb * y).astype(o_ref.dtype)
    tail = u[tm - V7X_SUBLANES:tm]
    carry_ref[...] = tail

    @pl.when(m % tps == tps - 1)
    def _():
        hist_ref[m // tps] = tail


def _convb(h, w3, cw, B, tn, tm):
    M, K = h.shape
    C = cw.shape[1]
    S = M // B
    tps = S // tm
    nb = C // tn
    return pl.pallas_call(
        functools.partial(_convb_body, K=K, chunk=_pick_tile(K, 512, V7X_BF16_ROWS),
                          tm=tm, tps=tps),
        grid=(nb, M // tm),
        in_specs=[pl.BlockSpec((tm, K), lambda n, m: (m, 0)),
                  pl.BlockSpec((K, tn), lambda n, m: (0, n)),
                  pl.BlockSpec((K, tn), lambda n, m: (0, n + nb)),
                  pl.BlockSpec((K, tn), lambda n, m: (0, n + 2 * nb)),
                  pl.BlockSpec((CONV_W, tn), lambda n, m: (0, n))],
        out_specs=[pl.BlockSpec((tm, tn), lambda n, m: (m, n)),
                   pl.BlockSpec((B, V7X_SUBLANES, tn), lambda n, m: (0, 0, n))],
        out_shape=[jax.ShapeDtypeStruct((M, C), BF16),
                   jax.ShapeDtypeStruct((B, V7X_SUBLANES, C), F32)],
        scratch_shapes=[pltpu.VMEM((3, K, tn), BF16),
                        pltpu.VMEM((V7X_SUBLANES, tn), F32)],
        compiler_params=_cparams(2), name="convb")(h, w3, w3, w3, cw)


def _cumsum_body(x_ref, o_ref, *, S):
    x = x_ref[...]
    lane = lax.broadcasted_iota(jnp.int32, x.shape, 1)
    s = 1
    while s < S:
        x = x + jnp.where(lane >= s, pltpu.roll(x, s, axis=1), 0.0)
        s *= 2
    o_ref[...] = x


def _cumsum_lanes(x):
    B, H, S = x.shape
    return pl.pallas_call(
        functools.partial(_cumsum_body, S=S),
        grid=(B,),
        in_specs=[pl.BlockSpec((None, H, S), lambda b: (b, 0, 0))],
        out_specs=pl.BlockSpec((None, H, S), lambda b: (b, 0, 0)),
        out_shape=jax.ShapeDtypeStruct((B, H, S), F32),
        compiler_params=_cparams(1), name="cumsum")(x)


def _attn_body(q_ref, k_ref, v_ref, c_ref, o_ref, m_ref, l_ref, acc_ref, *, tq, scale):
    qi = pl.program_id(2)
    q = q_ref[...]
    cq = c_ref[qi][:, 0:1]
    m_ref[...] = jnp.full(m_ref.shape, -jnp.inf, F32)
    l_ref[...] = jnp.zeros_like(l_ref)
    acc_ref[...] = jnp.zeros_like(acc_ref)

    def tile(kt, masked):
        r = pl.multiple_of(kt * tq, tq)
        kb = k_ref[pl.ds(r, tq), :].astype(BF16)
        vb = v_ref[pl.ds(r, tq), :].astype(BF16)
        s = lax.dot_general(q, kb, (((1,), (1,)), ((), ())),
                            preferred_element_type=F32) * scale
        s = s + (cq - c_ref[kt])
        if masked:
            row = lax.broadcasted_iota(jnp.int32, (tq, tq), 0)
            col = lax.broadcasted_iota(jnp.int32, (tq, tq), 1)
            s = jnp.where(row >= col, s, -jnp.inf)
        m_old = m_ref[...]
        m_new = jnp.maximum(m_old, jnp.max(s, axis=-1, keepdims=True))
        alpha = jnp.exp(m_old - m_new)
        p = jnp.exp(s - m_new)
        l_ref[...] = alpha * l_ref[...] + jnp.sum(p, axis=-1, keepdims=True)
        acc_ref[...] = alpha * acc_ref[...] + jnp.dot(
            p.astype(BF16), vb, preferred_element_type=F32)
        m_ref[...] = m_new

    def body(kt, c):
        tile(kt, False)
        return c

    lax.fori_loop(0, qi, body, 0)
    tile(qi, True)
    o_ref[...] = (acc_ref[...] / l_ref[...]).astype(o_ref.dtype)


def _attn_prompt(q, k, v, c, B, H, Dh, tq):
    M = q.shape[0]
    S = M // B
    nq = S // tq
    c4 = c.reshape(B, H, nq, 1, tq)
    return pl.pallas_call(
        functools.partial(_attn_body, tq=tq, scale=Dh ** -0.5),
        grid=(B, H, nq),
        in_specs=[pl.BlockSpec((tq, Dh), lambda b, h, i: (b * nq + i, h)),
                  pl.BlockSpec((S, Dh), lambda b, h, i: (b, h)),
                  pl.BlockSpec((S, Dh), lambda b, h, i: (b, h)),
                  pl.BlockSpec((None, None, nq, 1, tq), lambda b, h, i: (b, h, 0, 0, 0))],
        out_specs=pl.BlockSpec((tq, Dh), lambda b, h, i: (b * nq + i, h)),
        out_shape=jax.ShapeDtypeStruct((M, H * Dh), BF16),
        scratch_shapes=[pltpu.VMEM((tq, 1), F32), pltpu.VMEM((tq, 1), F32),
                        pltpu.VMEM((tq, Dh), F32)],
        compiler_params=_cparams(3), name="attn_prompt")(q, k, v, c4)


def _pool_body(x_ref, gpre_ref, gpost_ref, gnext_ref, sc_ref, w_ref,
               xo_ref, ho_ref, hist_ref, carry_ref, *, tm, tps, PG):
    m = pl.program_id(0)
    pos_tile = m % tps

    @pl.when(pos_tile == 0)
    def _():
        carry_ref[...] = jnp.zeros_like(carry_ref)

    x = x_ref[...]
    h = _rms(x, gpre_ref[...])
    pos = pos_tile * tm + lax.broadcasted_iota(jnp.int32, (tm, 1), 0)
    outs = []
    for g, w in enumerate(POOL_WINDOWS):
        cols = slice(g * PG, (g + 1) * PG)
        hg = h[:, cols]
        a = hg
        lvl, k = 0, 1
        while k < w:
            c8 = carry_ref[lvl, :, cols]
            carry_ref[lvl, :, cols] = a[tm - V7X_SUBLANES:tm]
            a = a + _shift_rows(a, c8, k)
            lvl += 1
            k *= 2
        cnt = jnp.minimum(pos + 1, w).astype(F32)
        pooled = a / cnt - hg
        outs.append(jnp.dot(pooled.astype(BF16), w_ref[g], preferred_element_type=F32))
    mixed = jnp.concatenate(outs, axis=1) * sc_ref[...]
    xn = x + _rms(mixed, gpost_ref[...])
    xo_ref[...] = xn
    ho_ref[...] = _rms(xn, gnext_ref[...]).astype(BF16)

    @pl.when(pos_tile == tps - 1)
    def _():
        hist_ref[...] = h[tm - 2 * V7X_SUBLANES:tm]


def _pool_prompt(x, B, g_pre, g_post, g_next, scale, wpool_bf16):
    M, D = x.shape
    G, PG, _ = wpool_bf16.shape
    S = M // B
    tm = _pick_tile(S, POOL_ROW_TILE, 2 * V7X_SUBLANES)
    tps = S // tm
    row = pl.BlockSpec((tm, D), lambda m: (m, 0))
    gain = pl.BlockSpec((1, D), lambda m: (0, 0))
    nlev = len(POOL_WINDOWS)
    return pl.pallas_call(
        functools.partial(_pool_body, tm=tm, tps=tps, PG=PG),
        grid=(M // tm,),
        in_specs=[row, gain, gain, gain, gain,
                  pl.BlockSpec((G, PG, PG), lambda m: (0, 0, 0),
                               pipeline_mode=pl.Buffered(1))],
        out_specs=[row, row,
                   pl.BlockSpec((None, 2 * V7X_SUBLANES, D), lambda m: (m // tps, 0, 0))],
        out_shape=[jax.ShapeDtypeStruct((M, D), F32),
                   jax.ShapeDtypeStruct((M, D), BF16),
                   jax.ShapeDtypeStruct((B, 2 * V7X_SUBLANES, D), F32)],
        scratch_shapes=[pltpu.VMEM((nlev, V7X_SUBLANES, D), F32)],
        compiler_params=_cparams(1), name="pool_prompt")(
            x, g_pre.reshape(1, D), g_post.reshape(1, D), g_next.reshape(1, D),
            scale.reshape(1, D), wpool_bf16)


def _ffn_up_body(h_ref, wg_ref, wv_ref, cwg_ref, cwv_ref, bg_ref, bv_ref,
                 o_ref, hg_ref, hv_ref, wb_ref, carry_ref, *, K, chunk, tm, tps, NF):
    n = pl.program_id(0)
    m = pl.program_id(1)

    @pl.when(n < NF)
    def _():
        @pl.when(m == 0)
        def _():
            _cast_rows(wg_ref, wb_ref.at[0], K, chunk)
            _cast_rows(wv_ref, wb_ref.at[1], K, chunk)

        @pl.when(m % tps == 0)
        def _():
            carry_ref[...] = jnp.zeros_like(carry_ref)

        h = h_ref[...]
        gate = jnp.dot(h, wb_ref[0], preferred_element_type=F32)
        val = jnp.dot(h, wb_ref[1], preferred_element_type=F32)
        gc = _causal_conv3(gate, carry_ref[0], cwg_ref[...]) + bg_ref[...]
        vc = _causal_conv3(val, carry_ref[1], cwv_ref[...]) + bv_ref[...]
        o_ref[...] = (_silu(gc) * vc).astype(o_ref.dtype)
        gtail = gate[tm - V7X_SUBLANES:tm]
        vtail = val[tm - V7X_SUBLANES:tm]
        carry_ref[0] = gtail
        carry_ref[1] = vtail

        @pl.when(m % tps == tps - 1)
        def _():
            hg_ref[m // tps] = gtail
            hv_ref[m // tps] = vtail

    @pl.when(n >= NF)
    def _():
        o_ref[...] = jnp.zeros_like(o_ref)


def _ffn_up(h, w_up, layer, cw, cb, B, F, FP, tn, tm):
    M, K = h.shape
    S = M // B
    tps = S // tm
    NF = F // tn
    last = NF - 1
    gsel = lambda n, m: (layer, 0, jnp.minimum(n, last))
    vsel = lambda n, m: (layer, 0, jnp.minimum(n, last) + NF)
    g2 = lambda n, m: (0, jnp.minimum(n, last))
    v2 = lambda n, m: (0, jnp.minimum(n, last) + NF)
    hist = pl.BlockSpec((B, V7X_SUBLANES, tn), lambda n, m: (0, 0, jnp.minimum(n, last)))
    return pl.pallas_call(
        functools.partial(_ffn_up_body, K=K, chunk=_pick_tile(K, 512, V7X_BF16_ROWS),
                          tm=tm, tps=tps, NF=NF),
        grid=(FP // tn, M // tm),
        in_specs=[pl.BlockSpec((tm, K), lambda n, m: (m, 0)),
                  pl.BlockSpec((None, K, tn), gsel),
                  pl.BlockSpec((None, K, tn), vsel),
                  pl.BlockSpec((CONV_W, tn), g2), pl.BlockSpec((CONV_W, tn), v2),
                  pl.BlockSpec((1, tn), g2), pl.BlockSpec((1, tn), v2)],
        out_specs=[pl.BlockSpec((tm, tn), lambda n, m: (m, n)), hist, hist],
        out_shape=[jax.ShapeDtypeStruct((M, FP), BF16),
                   jax.ShapeDtypeStruct((B, V7X_SUBLANES, F), F32),
                   jax.ShapeDtypeStruct((B, V7X_SUBLANES, F), F32)],
        scratch_shapes=[pltpu.VMEM((2, K, tn), BF16),
                        pltpu.VMEM((2, V7X_SUBLANES, tn), F32)],
        compiler_params=_cparams(2), name="ffn_up")(h, w_up, w_up, cw, cw, cb, cb)


def _ffn_down_body(a_ref, w_ref, o_ref, wb_ref, acc_ref, *, F, tk, nk, chunk):
    k = pl.program_id(1)
    m = pl.program_id(2)

    @pl.when(m == 0)
    def _():
        _cast_rows(w_ref, wb_ref, tk, chunk, row0=k * tk, valid_rows=F)

    d = jnp.dot(a_ref[...], wb_ref[...], preferred_element_type=F32)

    @pl.when(k == 0)
    def _():
        acc_ref[m] = d

    @pl.when(jnp.logical_and(k > 0, k < nk - 1))
    def _():
        acc_ref[m] = acc_ref[m] + d

    @pl.when(k == nk - 1)
    def _():
        o_ref[...] = acc_ref[m] + d


def _ffn_down(act, w_down, layer, F, tk, tn, tm):
    M, FP = act.shape
    N = w_down.shape[-1]
    nk = FP // tk
    kmax = (F - 1) // tk
    return pl.pallas_call(
        functools.partial(_ffn_down_body, F=F, tk=tk, nk=nk,
                          chunk=_pick_tile(tk, 512, V7X_BF16_ROWS)),
        grid=(N // tn, nk, M // tm),
        in_specs=[pl.BlockSpec((tm, tk), lambda n, k, m: (m, k)),
                  pl.BlockSpec((None, tk, tn),
                               lambda n, k, m: (layer, jnp.minimum(k, kmax), n))],
        out_specs=pl.BlockSpec((tm, tn), lambda n, k, m: (jnp.where(k == nk - 1, m, 0), n)),
        out_shape=jax.ShapeDtypeStruct((M, N), F32),
        scratch_shapes=[pltpu.VMEM((tk, tn), BF16),
                        pltpu.VMEM((M // tm, tm, tn), F32)],
        compiler_params=_cparams(3), name="ffn_down")(act, w_down)


def _whole(shape):
    nd = len(shape)
    return pl.BlockSpec(shape, lambda *_: (0,) * nd)


def _s_even_glue_body(f_ref, gb_ref, gc_ref, hv_ref, h0_ref, h1_ref, cw_ref, bf_ref,
                      lf_ref, cg_ref, u_ref):
    x = f_ref[...] + bf_ref[...]
    lf_ref[...] = jnp.minimum(x, 0.0) - jnp.log1p(jnp.exp(-jnp.abs(x)))
    u = gc_ref[...] * hv_ref[...]
    w = cw_ref[...]
    y = h0_ref[...] * w[0:1]
    y = y + h1_ref[...] * w[1:2]
    y = y + u * w[2:3]
    cg_ref[...] = (gb_ref[...] * y).astype(cg_ref.dtype)
    u_ref[...] = u


def _s_even_glue(f, gb, gc, hv, h0, h1, cw, bf):
    R, C = gb.shape
    H = f.shape[1]
    args = (f, gb, gc, hv, h0, h1, cw, bf.reshape(1, H))
    return pl.pallas_call(
        _s_even_glue_body, grid=(1,),
        in_specs=[_whole(a.shape) for a in args],
        out_specs=[_whole((R, H)), _whole((R, C)), _whole((R, C))],
        out_shape=[jax.ShapeDtypeStruct((R, H), F32), jax.ShapeDtypeStruct((R, C), BF16),
                   jax.ShapeDtypeStruct((R, C), F32)],
        compiler_params=_cparams(1), name="sample_even_glue")(*args)


def _s_ffn_glue_body(up_ref, h0_ref, h1_ref, cw_ref, cb_ref, o_ref, *, F, FP):
    w = cw_ref[...]
    y = h0_ref[...] * w[0:1]
    y = y + h1_ref[...] * w[1:2]
    y = y + up_ref[...] * w[2:3] + cb_ref[...]
    act = (_silu(y[:, :F]) * y[:, F:]).astype(o_ref.dtype)
    o_ref[:, :F] = act
    if FP > F:
        o_ref[:, F:] = jnp.zeros((act.shape[0], FP - F), o_ref.dtype)


def _s_ffn_glue(up, h0, h1, cw, cb, F, FP):
    R = up.shape[0]
    args = (up, h0, h1, cw, cb)
    return pl.pallas_call(
        functools.partial(_s_ffn_glue_body, F=F, FP=FP), grid=(1,),
        in_specs=[_whole(a.shape) for a in args],
        out_specs=_whole((R, FP)),
        out_shape=jax.ShapeDtypeStruct((R, FP), BF16),
        compiler_params=_cparams(1), name="sample_ffn_glue")(*args)


def _s_pool_body(x_ref, st_ref, gpre_ref, gpost_ref, gnext_ref, sc_ref, w_ref,
                 xo_ref, ho_ref, hs_ref, *, PG, NH, past):
    x = x_ref[...]
    h = _rms(x, gpre_ref[...])
    hs_ref[...] = h
    outs = []
    for g, w in enumerate(POOL_WINDOWS):
        cols = slice(g * PG, (g + 1) * PG)
        hg = h[:, cols]
        a = hg
        for i in range(1, w):
            a = a + st_ref[NH - i, :, cols]
        pooled = a / float(min(past + 1, w)) - hg
        outs.append(jnp.dot(pooled.astype(BF16), w_ref[g], preferred_element_type=F32))
    mixed = jnp.concatenate(outs, axis=1) * sc_ref[...]
    xn = x + _rms(mixed, gpost_ref[...])
    xo_ref[...] = xn
    ho_ref[...] = _rms(xn, gnext_ref[...]).astype(BF16)


def _s_pool(x, st, g_pre, g_post, g_next, scale, wpool_bf16, past):
    R, D = x.shape
    NH = st.shape[0]
    PG = wpool_bf16.shape[1]
    args = (x, st, g_pre.reshape(1, D), g_post.reshape(1, D), g_next.reshape(1, D),
            scale.reshape(1, D), wpool_bf16)
    return pl.pallas_call(
        functools.partial(_s_pool_body, PG=PG, NH=NH, past=past), grid=(1,),
        in_specs=[_whole(a.shape) for a in args],
        out_specs=[_whole((R, D)), _whole((R, D)), _whole((R, D))],
        out_shape=[jax.ShapeDtypeStruct((R, D), F32), jax.ShapeDtypeStruct((R, D), BF16),
                   jax.ShapeDtypeStruct((R, D), F32)],
        compiler_params=_cparams(1), name="sample_pool")(*args)


def _s_attn_body(pt_ref, q_ref, kn_ref, vn_ref, lfn_ref, dmask_ref, ck_ref, cv_ref, clf_ref,
                 o_ref, m_ref, l_ref, acc_ref, run_ref, *, H, Dh, PAGE, scale):
    p = pl.program_id(1)
    n_pages = pl.num_programs(1)
    PH = PAGE * H
    q = q_ref[...]

    @pl.when(p == 0)
    def _():
        m_ref[...] = jnp.full(m_ref.shape, -jnp.inf, F32)
        l_ref[...] = jnp.zeros_like(l_ref)
        acc_ref[...] = jnp.zeros_like(acc_ref)
        run_ref[...] = lfn_ref[...]

    lf = clf_ref[...]
    lane = lax.broadcasted_iota(jnp.int32, lf.shape, 1)
    suf = lf
    tot = lf
    s = H
    while s < PH:
        suf = suf + jnp.where(lane + s < PH, pltpu.roll(suf, PH - s, axis=1), 0.0)
        tot = tot + pltpu.roll(tot, s, axis=1)
        s *= 2
    bias = (suf - lf) + run_ref[...]
    run_ref[...] = run_ref[...] + tot

    kb = ck_ref[...].reshape(PH, Dh).astype(BF16)
    vb = cv_ref[...].reshape(PH, Dh).astype(BF16)
    sc = lax.dot_general(q.astype(BF16), kb, (((1,), (1,)), ((), ())),
                         preferred_element_type=F32) * scale
    sc = sc + bias + dmask_ref[...]
    m_old = m_ref[...]
    m_new = jnp.maximum(m_old, jnp.max(sc, axis=-1, keepdims=True))
    alpha = jnp.exp(m_old - m_new)
    pr = jnp.exp(sc - m_new)
    l_ref[...] = alpha * l_ref[...] + jnp.sum(pr, axis=-1, keepdims=True)
    acc_ref[...] = alpha * acc_ref[...] + jnp.dot(pr.astype(BF16), vb,
                                                  preferred_element_type=F32)
    m_ref[...] = m_new

    @pl.when(p == n_pages - 1)
    def _():
        sn = jnp.sum(q * kn_ref[...], axis=-1, keepdims=True) * scale
        m_o = m_ref[...]
        m_n = jnp.maximum(m_o, sn)
        a = jnp.exp(m_o - m_n)
        pn = jnp.exp(sn - m_n)
        l = a * l_ref[...] + pn
        acc = a * acc_ref[...] + pn * vn_ref[...]
        o_ref[...] = acc / l


def _s_attn(q, kn, vn, lfn, cache_k, cache_v, cache_lf, layer, page_table):
    DB, H, Dh = q.shape
    PAGE = cache_k.shape[2]
    NP = page_table.shape[1]
    PH = PAGE * H
    n_pool = cache_k.shape[1]
    clf_flat = cache_lf.reshape(cache_lf.shape[0], n_pool, 1, PH)
    lfn_flat = jnp.tile(lfn[:, None, :], (1, 1, PAGE))
    head_of_lane = jnp.arange(PH, dtype=jnp.int32) % H
    dmask = jnp.where(head_of_lane[None, :] == jnp.arange(H, dtype=jnp.int32)[:, None],
                      0.0, -jnp.inf).astype(F32)
    pt_flat = page_table.reshape(-1).astype(jnp.int32)

    def page(b, p, pt):
        return pt[b * NP + (NP - 1 - p)]

    per_req = pl.BlockSpec((None, H, Dh), lambda b, p, pt: (b, 0, 0))
    grid_spec = pltpu.PrefetchScalarGridSpec(
        num_scalar_prefetch=1, grid=(DB, NP),
        in_specs=[per_req, per_req, per_req,
                  pl.BlockSpec((None, 1, PH), lambda b, p, pt: (b, 0, 0)),
                  pl.BlockSpec((H, PH), lambda b, p, pt: (0, 0)),
                  pl.BlockSpec((None, None, PAGE, H, Dh),
                               lambda b, p, pt: (layer, page(b, p, pt), 0, 0, 0)),
                  pl.BlockSpec((None, None, PAGE, H, Dh),
                               lambda b, p, pt: (layer, page(b, p, pt), 0, 0, 0)),
                  pl.BlockSpec((None, None, 1, PH),
                               lambda b, p, pt: (layer, page(b, p, pt), 0, 0))],
        out_specs=per_req,
        scratch_shapes=[pltpu.VMEM((H, 1), F32), pltpu.VMEM((H, 1), F32),
                        pltpu.VMEM((H, Dh), F32), pltpu.VMEM((1, PH), F32)])
    return pl.pallas_call(
        functools.partial(_s_attn_body, H=H, Dh=Dh, PAGE=PAGE, scale=Dh ** -0.5),
        grid_spec=grid_spec,
        out_shape=jax.ShapeDtypeStruct((DB, H, Dh), F32),
        compiler_params=_cparams(2), name="attn_sample")(
            pt_flat, q, kn, vn, lfn_flat, dmask, cache_k, cache_v, clf_flat)


def _pad_rows(a, rows):
    return jnp.pad(a, ((0, rows - a.shape[0]),) + ((0, 0),) * (a.ndim - 1))


def kernel(x_prompt, x_sample, cache_k, cache_v, cache_logf, page_table, state_conv_b,
           state_pool, state_ffn_conv, w_in_mix, b_forget, w_conv_b, w_out_mix, w_pool,
           pool_scale, w_up, w_ffn_conv, b_ffn_conv, w_down, g_mix_pre, g_mix_post,
           g_ffn_pre, g_ffn_post):
    B, S, D = x_prompt.shape
    DB = x_sample.shape[0]
    assert x_sample.shape[1] == 1
    depth = w_up.shape[0]
    H, Dh = cache_k.shape[3], cache_k.shape[4]
    WA = H * Dh
    WB = D - WA
    F = w_down.shape[1]
    IN_W = w_in_mix.shape[-1]
    assert IN_W == 3 * WA + H + 3 * WB
    PAGE = cache_k.shape[2]
    past = page_table.shape[1] * PAGE
    NH = state_pool.shape[2]
    M = B * S
    R = SAMPLE_ROWS
    assert DB <= R

    tm = _pick_tile(S, PROMPT_ROW_TILE, V7X_BF16_ROWS)
    tn = 2 * V7X_MXU_DIM
    tn_a = _pick_tile(WA, tn, V7X_LANES)
    tn_b = _pick_tile(WB, V7X_MXU_DIM, V7X_LANES)
    tn_d = _pick_tile(D, tn, V7X_LANES)
    tn_f = _pick_tile(F, V7X_MXU_DIM, V7X_LANES)
    tk_f = _round_up(-(-F // 4), tn_f)
    FP = 4 * tk_f
    tq = _pick_tile(S, ATTN_TILE, V7X_LANES)
    fblk = (3 * WA) // V7X_LANES
    assert (3 * WA) % V7X_LANES == 0 and H <= V7X_LANES

    xp = x_prompt.reshape(M, D)
    xs = _pad_rows(x_sample.reshape(DB, D), R)
    wpool_bf16 = w_pool.astype(BF16)

    hp = _rms_cast(xp, g_mix_pre[0])
    hs = _rms_cast(xs, g_mix_pre[0])

    kp_l, vp_l, lp_l, ks_l, vs_l, ls_l = [], [], [], [], [], []
    cbp_l, cbs_l, plp_l, pls_l, ffp_l, ffs_l = [], [], [], [], [], []
    for layer in range(depth):
        j = layer // 2
        g_next = g_ffn_pre[layer]
        if layer % 2 == 0:
            q = _mm(hp, w_in_mix, (j,), 0, WA, tn_a, tm, BF16, "proj_q")
            k = _mm(hp, w_in_mix, (j,), WA // tn_a, WA, tn_a, tm, F32, "proj_k")
            v = _mm(hp, w_in_mix, (j,), 2 * WA // tn_a, WA, tn_a, tm, F32, "proj_v")
            lf = _logf(hp, w_in_mix, (j,), fblk, b_forget[j], tm)
            c = _cumsum_lanes(lf.reshape(B, S, H).transpose(0, 2, 1))
            att = _attn_prompt(q, k, v, c, B, H, Dh, tq)
            w3 = lax.slice(w_in_mix[j], (0, 3 * WA + H), (D, IN_W))
            cg, ub = _convb(hp, w3, w_conv_b[j], B, tn_b, tm)
            fp = _mm2(att, cg, w_out_mix, (j,), tn_d, tm, "out_proj")
            kp_l.append(k.reshape(B, S, H, Dh))
            vp_l.append(v.reshape(B, S, H, Dh))
            lp_l.append(lf.reshape(B, S, H))
            cbp_l.append(ub[:, V7X_SUBLANES - (CONV_W - 1):])
            zs = _mm(hs, w_in_mix, (j,), 0, IN_W, tn, R, F32, "proj_sample")
            o = 3 * WA + H
            st = _pad_rows(state_conv_b[j], R)
            lfs, cgs, us = _s_even_glue(
                zs[:, 3 * WA:o], zs[:, o:o + WB], zs[:, o + WB:o + 2 * WB],
                zs[:, o + 2 * WB:], st[:, 0], st[:, 1], w_conv_b[j], b_forget[j])
            qs = zs[:DB, :WA].reshape(DB, H, Dh)
            ksn = zs[:DB, WA:2 * WA].reshape(DB, H, Dh)
            vsn = zs[:DB, 2 * WA:3 * WA].reshape(DB, H, Dh)
            att_s = _s_attn(qs, ksn, vsn, lfs[:DB], cache_k, cache_v, cache_logf, j,
                            page_table)
            att_s = _pad_rows(att_s.reshape(DB, WA), R).astype(BF16)
            fs = _mm2(att_s, cgs, w_out_mix, (j,), tn_d, R, "out_proj_sample")
            ks_l.append(ksn[:, None])
            vs_l.append(vsn[:, None])
            ls_l.append(lfs[:DB, None])
            cbs_l.append(jnp.stack([state_conv_b[j][:, 1], us[:DB]], axis=1))
            xp, hp = _resnorm(xp, fp, g_mix_post[layer], g_next)
            xs, hs = _resnorm(xs, fs, g_mix_post[layer], g_next)
        else:
            xp, hp, ph = _pool_prompt(xp, B, g_mix_pre[layer], g_mix_post[layer], g_next,
                                      pool_scale[j], wpool_bf16[j])
            plp_l.append(ph[:, 2 * V7X_SUBLANES - NH:])
            st = _pad_rows(state_pool[j], R).transpose(1, 0, 2)
            xs, hs, hs_f32 = _s_pool(xs, st, g_mix_pre[layer], g_mix_post[layer], g_next,
                                     pool_scale[j], wpool_bf16[j], past)
            pls_l.append(jnp.concatenate([state_pool[j][:, 1:], hs_f32[:DB, None]], axis=1))

        cw = w_ffn_conv[layer]
        cb = b_ffn_conv[layer].reshape(1, 2 * F)
        act, hg, hv = _ffn_up(hp, w_up, layer, cw, cb, B, F, FP, tn_f, tm)
        fp = _ffn_down(act, w_down, layer, F, tk_f, tn_d, tm)
        ffp_l.append(jnp.concatenate([hg[:, V7X_SUBLANES - 2:], hv[:, V7X_SUBLANES - 2:]],
                                     axis=-1))
        ups = _mm(hs, w_up, (layer,), 0, 2 * F, tn_f, R, F32, "ffn_up_sample")
        sf = _pad_rows(state_ffn_conv[layer], R)
        act_s = _s_ffn_glue(ups, sf[:, 0], sf[:, 1], cw, cb, F, FP)
        fs = _ffn_down(act_s, w_down, layer, F, tk_f, tn_d, R)
        ffs_l.append(jnp.stack([state_ffn_conv[layer][:, 1], ups[:DB]], axis=1))

        if layer + 1 < depth and (layer + 1) % 2 == 0:
            g_after = g_mix_pre[layer + 1]
        else:
            g_after = None
        xp, hp = _resnorm(xp, fp, g_ffn_post[layer], g_after)
        xs, hs = _resnorm(xs, fs, g_ffn_post[layer], g_after)

    return (xp.reshape(B, S, D), xs[:DB].reshape(DB, 1, D),
            jnp.stack(kp_l), jnp.stack(vp_l), jnp.stack(lp_l),
            jnp.stack(ks_l), jnp.stack(vs_l), jnp.stack(ls_l),
            jnp.stack(cbp_l), jnp.stack(cbs_l),
            jnp.stack(plp_l), jnp.stack(pls_l),
            jnp.stack(ffp_l), jnp.stack(ffs_l))
```

```python
import functools

import jax
import jax.numpy as jnp
from jax import lax
from jax.experimental import pallas as pl
from jax.experimental.pallas import tpu as pltpu

F32 = jnp.float32
BF16 = jnp.bfloat16

RMS_EPS = 1e-6
POOL_WINDOWS = (2, 4, 8, 16)
CONV_W = 3

V7X_LANES = 128
V7X_SUBLANES = 8
V7X_BF16_ROWS = 16
V7X_MXU_DIM = 256
V7X_VMEM_LIMIT_BYTES = 58 * 1024 * 1024

SAMPLE_ROWS = V7X_BF16_ROWS

PROMPT_ROW_TILE = 1024
ATTN_TILE = 512
ATTN_HEADS_PER_STEP = 4
ATTN_ROW_BLOCK = 256
POOL_ROW_TILE = 256
SAMPLE_PAGES_PER_STEP = 4

_NT = (((1,), (1,)), ((), ()))


def _cparams(n_axes):
    return pltpu.CompilerParams(
        dimension_semantics=("arbitrary",) * n_axes,
        vmem_limit_bytes=V7X_VMEM_LIMIT_BYTES)


def _pick_tile(n, pref, align):
    t = min(pref, n)
    t -= t % align
    while t >= align:
        if n % t == 0:
            return t
        t -= align
    return n


def _round_up(x, m):
    return (x + m - 1) // m * m


def _rms(x, g):
    ms = jnp.mean(x * x, axis=-1, keepdims=True)
    return (x * lax.rsqrt(ms + RMS_EPS)) * g


def _cast_rows(load, dst_ref, nrows, chunk, row0=None, valid_rows=None):
    def body(i, c):
        r = pl.multiple_of(i * chunk, chunk)
        w = load(pl.ds(r, chunk))
        if valid_rows is not None:
            gr = row0 + r + lax.broadcasted_iota(jnp.int32, (chunk, 1), 0)
            w = jnp.where(gr < valid_rows, w, 0.0)
        dst_ref[pl.ds(r, chunk), :] = w.astype(BF16)
        return c
    lax.fori_loop(0, nrows // chunk, body, 0)


def _cast_wt(w_ref, dst_ref, nrows):
    _cast_rows(lambda rows: w_ref[0, rows, :], dst_ref, nrows,
               _pick_tile(nrows, 64, V7X_BF16_ROWS))


def _dot_nt(a, bt):
    return lax.dot_general(a, bt, _NT, preferred_element_type=F32)


def _shift_rows(a, carry8, k):
    rows = a.shape[0]
    r = pltpu.roll(a, k, axis=0)
    row8 = lax.broadcasted_iota(jnp.int32, (V7X_SUBLANES, 1), 0)
    top = jnp.where(row8 < k, pltpu.roll(carry8, k, axis=0), r[0:V7X_SUBLANES])
    if rows == V7X_SUBLANES:
        return top
    return jnp.concatenate([top, r[V7X_SUBLANES:]], axis=0)


def _causal_conv3(u, carry8, w):
    y = _shift_rows(u, carry8, 2) * w[0:1]
    y = y + _shift_rows(u, carry8, 1) * w[1:2]
    return y + u * w[2:3]


def _conv3_step(h0, h1, u, w):
    y = h0 * w[0:1]
    y = y + h1 * w[1:2]
    return y + u * w[2:3]


def _silu(x):
    return x * (1.0 / (1.0 + jnp.exp(-x)))


def _log_sigmoid(x):
    return jnp.minimum(x, 0.0) - jnp.log1p(jnp.exp(-jnp.abs(x)))


def _wt_spec(tn, K, j, row0):
    return pl.BlockSpec(
        (pl.Element(1), pl.Element(tn), pl.Element(K)),
        lambda n, m: (j, pl.multiple_of(row0 + n * tn, V7X_SUBLANES), 0))


def _rms_cast_body(x_ref, g_ref, o_ref):
    o_ref[...] = _rms(x_ref[...], g_ref[...]).astype(o_ref.dtype)


def _rms_cast(x, g):
    M, D = x.shape
    tm = _pick_tile(M, 256, V7X_BF16_ROWS)
    return pl.pallas_call(
        _rms_cast_body,
        grid=(M // tm,),
        in_specs=[pl.BlockSpec((tm, D), lambda m: (m, 0)),
                  pl.BlockSpec((1, D), lambda m: (0, 0))],
        out_specs=pl.BlockSpec((tm, D), lambda m: (m, 0)),
        out_shape=jax.ShapeDtypeStruct((M, D), BF16),
        compiler_params=_cparams(1), name="rms_cast")(x, g.reshape(1, D))


def _resnorm_body(x_ref, f_ref, gp_ref, gn_ref, xo_ref, ho_ref):
    xn = x_ref[...] + _rms(f_ref[...], gp_ref[...])
    xo_ref[...] = xn
    ho_ref[...] = _rms(xn, gn_ref[...]).astype(BF16)


def _resnorm_last_body(x_ref, f_ref, gp_ref, xo_ref):
    xo_ref[...] = x_ref[...] + _rms(f_ref[...], gp_ref[...])


def _resnorm(x, f, g_post, g_next):
    M, D = x.shape
    tm = _pick_tile(M, 256, V7X_BF16_ROWS)
    row = pl.BlockSpec((tm, D), lambda m: (m, 0))
    gain = pl.BlockSpec((1, D), lambda m: (0, 0))
    if g_next is None:
        xo = pl.pallas_call(
            _resnorm_last_body, grid=(M // tm,),
            in_specs=[row, row, gain], out_specs=row,
            out_shape=jax.ShapeDtypeStruct((M, D), F32),
            compiler_params=_cparams(1), name="resnorm_last")(x, f, g_post.reshape(1, D))
        return xo, None
    return pl.pallas_call(
        _resnorm_body, grid=(M // tm,),
        in_specs=[row, row, gain, gain], out_specs=[row, row],
        out_shape=[jax.ShapeDtypeStruct((M, D), F32), jax.ShapeDtypeStruct((M, D), BF16)],
        compiler_params=_cparams(1), name="resnorm")(
            x, f, g_post.reshape(1, D), g_next.reshape(1, D))


def _proj_body(*refs, tn, aliased):
    if aliased:
        hp_ref, hs_ref, w_ref, _, op_ref, os_ref, wb_ref = refs
    else:
        hp_ref, hs_ref, w_ref, op_ref, os_ref, wb_ref = refs

    @pl.when(pl.program_id(1) == 0)
    def _():
        _cast_wt(w_ref, wb_ref, tn)
        os_ref[...] = _dot_nt(hs_ref[...], wb_ref[...])

    op_ref[...] = _dot_nt(hp_ref[...], wb_ref[...]).astype(op_ref.dtype)


def _proj(hp, hs, wt, j, row0, n_out, tn, tm, out_dtype, name, stack=None, prev=None):
    M, K = hp.shape
    R = hs.shape[0]
    in_specs = [pl.BlockSpec((tm, K), lambda n, m: (m, 0)),
                pl.BlockSpec((R, K), lambda n, m: (0, 0)),
                _wt_spec(tn, K, j, row0)]
    args = [hp, hs, wt]
    if stack is None:
        p_shape, p_spec = (M, n_out), pl.BlockSpec((tm, tn), lambda n, m: (m, n))
    else:
        p_shape = (stack, M, n_out)
        p_spec = pl.BlockSpec((None, tm, tn), lambda n, m: (j, m, n))
    aliases = {}
    if prev is not None:
        in_specs.append(pl.BlockSpec(memory_space=pl.ANY))
        args.append(prev)
        aliases = {3: 0}
    return pl.pallas_call(
        functools.partial(_proj_body, tn=tn, aliased=prev is not None),
        grid=(n_out // tn, M // tm),
        in_specs=in_specs,
        out_specs=[p_spec, pl.BlockSpec((R, tn), lambda n, m: (0, n))],
        out_shape=[jax.ShapeDtypeStruct(p_shape, out_dtype),
                   jax.ShapeDtypeStruct((R, n_out), F32)],
        scratch_shapes=[pltpu.VMEM((tn, K), BF16)],
        input_output_aliases=aliases,
        compiler_params=_cparams(2), name=name)(*args)


def _logf_body(hp_ref, hs_ref, w_ref, b_ref, op_ref, os_ref, wb_ref, *, H):
    @pl.when(pl.program_id(1) == 0)
    def _():
        _cast_wt(w_ref, wb_ref, H)
        os_ref[...] = _log_sigmoid(_dot_nt(hs_ref[...], wb_ref[...]) + b_ref[...])

    op_ref[...] = _log_sigmoid(_dot_nt(hp_ref[...], wb_ref[...]) + b_ref[...])


def _logf(hp, hs, wt, j, row0, b, tm):
    M, K = hp.shape
    R = hs.shape[0]
    H = b.shape[-1]
    return pl.pallas_call(
        functools.partial(_logf_body, H=H),
        grid=(1, M // tm),
        in_specs=[pl.BlockSpec((tm, K), lambda n, m: (m, 0)),
                  pl.BlockSpec((R, K), lambda n, m: (0, 0)),
                  _wt_spec(H, K, j, row0),
                  pl.BlockSpec((1, H), lambda n, m: (0, 0))],
        out_specs=[pl.BlockSpec((tm, H), lambda n, m: (m, 0)),
                   pl.BlockSpec((R, H), lambda n, m: (0, 0))],
        out_shape=[jax.ShapeDtypeStruct((M, H), F32), jax.ShapeDtypeStruct((R, H), F32)],
        scratch_shapes=[pltpu.VMEM((H, K), BF16)],
        compiler_params=_cparams(2), name="logf")(hp, hs, wt, b.reshape(1, H))


def _convb_body(hp_ref, hs_ref, wgb_ref, wgc_ref, whv_ref, cw_ref, s0_ref, s1_ref,
                o_ref, hist_ref, os_ref, us_ref, wb_ref, carry_ref, *, tn, tm, tps):
    m = pl.program_id(1)

    @pl.when(m == 0)
    def _():
        _cast_wt(wgb_ref, wb_ref.at[0], tn)
        _cast_wt(wgc_ref, wb_ref.at[1], tn)
        _cast_wt(whv_ref, wb_ref.at[2], tn)
        hs = hs_ref[...]
        us = _dot_nt(hs, wb_ref[1]) * _dot_nt(hs, wb_ref[2])
        y = _conv3_step(s0_ref[...], s1_ref[...], us, cw_ref[...])
        os_ref[...] = (_dot_nt(hs, wb_ref[0]) * y).astype(os_ref.dtype)
        us_ref[...] = us

    @pl.when(m % tps == 0)
    def _():
        carry_ref[...] = jnp.zeros_like(carry_ref)

    h = hp_ref[...]
    gb = _dot_nt(h, wb_ref[0])
    u = _dot_nt(h, wb_ref[1]) * _dot_nt(h, wb_ref[2])
    y = _causal_conv3(u, carry_ref[...], cw_ref[...])
    o_ref[...] = (gb * y).astype(o_ref.dtype)
    tail = u[tm - V7X_SUBLANES:tm]
    carry_ref[...] = tail

    @pl.when(m % tps == tps - 1)
    def _():
        hist_ref[m // tps] = tail


def _convb(hp, hs, wt, j, row0, cw, s0, s1, B, tn, tm):
    M, K = hp.shape
    R = hs.shape[0]
    C = cw.shape[1]
    S = M // B
    tps = S // tm
    small = pl.BlockSpec((R, tn), lambda n, m: (0, n))
    return pl.pallas_call(
        functools.partial(_convb_body, tn=tn, tm=tm, tps=tps),
        grid=(C // tn, M // tm),
        in_specs=[pl.BlockSpec((tm, K), lambda n, m: (m, 0)),
                  pl.BlockSpec((R, K), lambda n, m: (0, 0)),
                  _wt_spec(tn, K, j, row0), _wt_spec(tn, K, j, row0 + C),
                  _wt_spec(tn, K, j, row0 + 2 * C),
                  pl.BlockSpec((CONV_W, tn), lambda n, m: (0, n)), small, small],
        out_specs=[pl.BlockSpec((tm, tn), lambda n, m: (m, n)),
                   pl.BlockSpec((B, V7X_SUBLANES, tn), lambda n, m: (0, 0, n)),
                   small, small],
        out_shape=[jax.ShapeDtypeStruct((M, C), BF16),
                   jax.ShapeDtypeStruct((B, V7X_SUBLANES, C), F32),
                   jax.ShapeDtypeStruct((R, C), BF16),
                   jax.ShapeDtypeStruct((R, C), F32)],
        scratch_shapes=[pltpu.VMEM((3, tn, K), BF16),
                        pltpu.VMEM((V7X_SUBLANES, tn), F32)],
        compiler_params=_cparams(2), name="convb")(hp, hs, wt, wt, wt, cw, s0, s1)


def _out_proj_body(a_ref, b_ref, as_ref, bs_ref, w_ref, o_ref, os_ref, wb_ref,
                   *, KA, K, chunk):
    @pl.when(pl.program_id(1) == 0)
    def _():
        _cast_rows(lambda rows: w_ref[rows, :], wb_ref, K, chunk)
        os_ref[...] = (jnp.dot(as_ref[...], wb_ref[0:KA, :], preferred_element_type=F32)
                       + jnp.dot(bs_ref[...], wb_ref[KA:K, :], preferred_element_type=F32))

    o_ref[...] = (jnp.dot(a_ref[...], wb_ref[0:KA, :], preferred_element_type=F32)
                  + jnp.dot(b_ref[...], wb_ref[KA:K, :], preferred_element_type=F32))


def _out_proj(a, b, a_s, b_s, w, j, tn, tm):
    M, KA = a.shape
    KB = b.shape[1]
    K = KA + KB
    R = a_s.shape[0]
    N = w.shape[-1]
    return pl.pallas_call(
        functools.partial(_out_proj_body, KA=KA, K=K, chunk=_pick_tile(K, 512, V7X_BF16_ROWS)),
        grid=(N // tn, M // tm),
        in_specs=[pl.BlockSpec((tm, KA), lambda n, m: (m, 0)),
                  pl.BlockSpec((tm, KB), lambda n, m: (m, 0)),
                  pl.BlockSpec((R, KA), lambda n, m: (0, 0)),
                  pl.BlockSpec((R, KB), lambda n, m: (0, 0)),
                  pl.BlockSpec((None, K, tn), lambda n, m: (j, 0, n))],
        out_specs=[pl.BlockSpec((tm, tn), lambda n, m: (m, n)),
                   pl.BlockSpec((R, tn), lambda n, m: (0, n))],
        out_shape=[jax.ShapeDtypeStruct((M, N), F32), jax.ShapeDtypeStruct((R, N), F32)],
        scratch_shapes=[pltpu.VMEM((K, tn), BF16)],
        compiler_params=_cparams(2), name="out_proj")(a, b, a_s, b_s, w)


def _cumsum_body(x_ref, o_ref, *, S):
    x = x_ref[...]
    lane = lax.broadcasted_iota(jnp.int32, x.shape, 1)
    s = 1
    while s < S:
        x = x + jnp.where(lane >= s, pltpu.roll(x, s, axis=1), 0.0)
        s *= 2
    o_ref[...] = x


def _cumsum_lanes(x):
    B, H, S = x.shape
    return pl.pallas_call(
        functools.partial(_cumsum_body, S=S),
        grid=(B,),
        in_specs=[pl.BlockSpec((None, H, S), lambda b: (b, 0, 0))],
        out_specs=pl.BlockSpec((None, H, S), lambda b: (b, 0, 0)),
        out_shape=jax.ShapeDtypeStruct((B, H, S), F32),
        compiler_params=_cparams(1), name="cumsum")(x)


def _attn_body(q_ref, k_ref, v_ref, c_ref, o_ref, m_ref, acc_ref,
               *, tq, rb, HB, Dh, scale):
    qi = pl.program_id(2)
    m_ref[...] = jnp.full(m_ref.shape, -jnp.inf, F32)
    acc_ref[...] = jnp.zeros_like(acc_ref)

    def tile(hh, kt, diagonal):
        cols = slice(hh * Dh, (hh + 1) * Dh)
        r = pl.multiple_of(kt * tq, tq)
        kb = k_ref[pl.ds(r, tq), cols].astype(BF16)
        vb = jnp.concatenate([v_ref[pl.ds(r, tq), cols].astype(BF16),
                              jnp.ones((tq, Dh), BF16)], axis=1)
        bias = c_ref[hh, qi][:, 0:1] - c_ref[hh, kt]
        for rr in range(tq // rb):
            rows = slice(rr * rb, (rr + 1) * rb)
            s = _dot_nt(q_ref[rows, cols], kb) * scale + bias
            if diagonal:
                row = rr * rb + lax.broadcasted_iota(jnp.int32, (rb, tq), 0)
                col = lax.broadcasted_iota(jnp.int32, (rb, tq), 1)
                s = jnp.where(row >= col, s, -jnp.inf)
            m_old = m_ref[hh, rows]
            m_new = jnp.maximum(m_old, jnp.max(s, axis=-1, keepdims=True))
            p = jnp.exp(s - m_new)
            acc_ref[hh, rows] = jnp.exp(m_old - m_new) * acc_ref[hh, rows] + jnp.dot(
                p.astype(BF16), vb, preferred_element_type=F32)
            m_ref[hh, rows] = m_new

    def body(kt, c):
        for hh in range(HB):
            tile(hh, kt, False)
        return c

    lax.fori_loop(0, qi, body, 0)
    for hh in range(HB):
        tile(hh, qi, True)
    for hh in range(HB):
        a = acc_ref[hh]
        o_ref[:, hh * Dh:(hh + 1) * Dh] = (a[:, :Dh] / a[:, Dh:Dh + 1]).astype(o_ref.dtype)


def _attn_prompt(q, k, v, c, B, H, Dh, tq):
    k, kv_slot = k
    v, _ = v
    M = q.shape[0]
    S = M // B
    nq = S // tq
    HB = _pick_tile(H, ATTN_HEADS_PER_STEP, 1)
    W = HB * Dh
    c5 = c.reshape(B, H, nq, 1, tq)
    rb = _pick_tile(tq, ATTN_ROW_BLOCK, V7X_BF16_ROWS)
    return pl.pallas_call(
        functools.partial(_attn_body, tq=tq, rb=rb, HB=HB, Dh=Dh, scale=Dh ** -0.5),
        grid=(B, H // HB, nq),
        in_specs=[pl.BlockSpec((tq, W), lambda b, g, i: (b * nq + i, g)),
                  pl.BlockSpec((None, S, W), lambda b, g, i: (kv_slot, b, g)),
                  pl.BlockSpec((None, S, W), lambda b, g, i: (kv_slot, b, g)),
                  pl.BlockSpec((None, HB, nq, 1, tq), lambda b, g, i: (b, g, 0, 0, 0))],
        out_specs=pl.BlockSpec((tq, W), lambda b, g, i: (b * nq + i, g)),
        out_shape=jax.ShapeDtypeStruct((M, H * Dh), BF16),
        scratch_shapes=[pltpu.VMEM((HB, tq, 1), F32), pltpu.VMEM((HB, tq, 2 * Dh), F32)],
        compiler_params=_cparams(3), name="attn_prompt")(q, k, v, c5)


def _pool_body(x_ref, gpre_ref, gpost_ref, gnext_ref, sc_ref, w_ref,
               xo_ref, ho_ref, hist_ref, carry_ref, *, tm, tps, PG):
    m = pl.program_id(0)
    pos_tile = m % tps

    @pl.when(pos_tile == 0)
    def _():
        carry_ref[...] = jnp.zeros_like(carry_ref)

    x = x_ref[...]
    h = _rms(x, gpre_ref[...])
    pos = pos_tile * tm + lax.broadcasted_iota(jnp.int32, (tm, 1), 0)
    outs = []
    for g, w in enumerate(POOL_WINDOWS):
        cols = slice(g * PG, (g + 1) * PG)
        hg = h[:, cols]
        a = hg
        lvl, k = 0, 1
        while k < w:
            c8 = carry_ref[lvl, :, cols]
            carry_ref[lvl, :, cols] = a[tm - V7X_SUBLANES:tm]
            a = a + _shift_rows(a, c8, k)
            lvl += 1
            k *= 2
        cnt = jnp.minimum(pos + 1, w).astype(F32)
        pooled = a / cnt - hg
        outs.append(jnp.dot(pooled.astype(BF16), w_ref[g], preferred_element_type=F32))
    mixed = jnp.concatenate(outs, axis=1) * sc_ref[...]
    xn = x + _rms(mixed, gpost_ref[...])
    xo_ref[...] = xn
    ho_ref[...] = _rms(xn, gnext_ref[...]).astype(BF16)

    @pl.when(pos_tile == tps - 1)
    def _():
        hist_ref[...] = h[tm - 2 * V7X_SUBLANES:tm]


def _pool_prompt(x, B, g_pre, g_post, g_next, scale, wpool_bf16):
    M, D = x.shape
    G, PG, _ = wpool_bf16.shape
    S = M // B
    tm = _pick_tile(S, POOL_ROW_TILE, 2 * V7X_SUBLANES)
    tps = S // tm
    row = pl.BlockSpec((tm, D), lambda m: (m, 0))
    gain = pl.BlockSpec((1, D), lambda m: (0, 0))
    nlev = len(POOL_WINDOWS)
    return pl.pallas_call(
        functools.partial(_pool_body, tm=tm, tps=tps, PG=PG),
        grid=(M // tm,),
        in_specs=[row, gain, gain, gain, gain,
                  pl.BlockSpec((G, PG, PG), lambda m: (0, 0, 0),
                               pipeline_mode=pl.Buffered(1))],
        out_specs=[row, row,
                   pl.BlockSpec((None, 2 * V7X_SUBLANES, D), lambda m: (m // tps, 0, 0))],
        out_shape=[jax.ShapeDtypeStruct((M, D), F32),
                   jax.ShapeDtypeStruct((M, D), BF16),
                   jax.ShapeDtypeStruct((B, 2 * V7X_SUBLANES, D), F32)],
        scratch_shapes=[pltpu.VMEM((nlev, V7X_SUBLANES, D), F32)],
        compiler_params=_cparams(1), name="pool_prompt")(
            x, g_pre.reshape(1, D), g_post.reshape(1, D), g_next.reshape(1, D),
            scale.reshape(1, D), wpool_bf16)


def _ffn_up_body(h_ref, hs_ref, wg_ref, wv_ref, cwg_ref, cwv_ref, bg_ref, bv_ref,
                 o_ref, hg_ref, hv_ref, sg_ref, sv_ref, wb_ref, carry_ref,
                 *, K, chunk, tm, tps):
    m = pl.program_id(1)

    @pl.when(m == 0)
    def _():
        _cast_rows(lambda rows: wg_ref[rows, :], wb_ref.at[0], K, chunk)
        _cast_rows(lambda rows: wv_ref[rows, :], wb_ref.at[1], K, chunk)
        hs = hs_ref[...]
        sg_ref[...] = jnp.dot(hs, wb_ref[0], preferred_element_type=F32)
        sv_ref[...] = jnp.dot(hs, wb_ref[1], preferred_element_type=F32)

    @pl.when(m % tps == 0)
    def _():
        carry_ref[...] = jnp.zeros_like(carry_ref)

    h = h_ref[...]
    gate = jnp.dot(h, wb_ref[0], preferred_element_type=F32)
    val = jnp.dot(h, wb_ref[1], preferred_element_type=F32)
    gc = _causal_conv3(gate, carry_ref[0], cwg_ref[...]) + bg_ref[...]
    vc = _causal_conv3(val, carry_ref[1], cwv_ref[...]) + bv_ref[...]
    o_ref[...] = (_silu(gc) * vc).astype(o_ref.dtype)
    gtail = gate[tm - V7X_SUBLANES:tm]
    vtail = val[tm - V7X_SUBLANES:tm]
    carry_ref[0] = gtail
    carry_ref[1] = vtail

    @pl.when(m % tps == tps - 1)
    def _():
        hg_ref[m // tps] = gtail
        hv_ref[m // tps] = vtail


def _ffn_up(h, hs, w_up, layer, cw, cb, B, F, FP, tn, tm):
    M, K = h.shape
    R = hs.shape[0]
    S = M // B
    tps = S // tm
    NF = F // tn
    g2 = lambda n, m: (0, n)
    v2 = lambda n, m: (0, n + NF)
    hist = pl.BlockSpec((B, V7X_SUBLANES, tn), lambda n, m: (0, 0, n))
    samp = pl.BlockSpec((R, tn), lambda n, m: (0, n))
    return pl.pallas_call(
        functools.partial(_ffn_up_body, K=K, chunk=_pick_tile(K, 512, V7X_BF16_ROWS),
                          tm=tm, tps=tps),
        grid=(NF, M // tm),
        in_specs=[pl.BlockSpec((tm, K), lambda n, m: (m, 0)),
                  pl.BlockSpec((R, K), lambda n, m: (0, 0)),
                  pl.BlockSpec((None, K, tn), lambda n, m: (layer, 0, n)),
                  pl.BlockSpec((None, K, tn), lambda n, m: (layer, 0, n + NF)),
                  pl.BlockSpec((CONV_W, tn), g2), pl.BlockSpec((CONV_W, tn), v2),
                  pl.BlockSpec((1, tn), g2), pl.BlockSpec((1, tn), v2)],
        out_specs=[pl.BlockSpec((tm, tn), lambda n, m: (m, n)), hist, hist, samp, samp],
        out_shape=[jax.ShapeDtypeStruct((M, FP), BF16),
                   jax.ShapeDtypeStruct((B, V7X_SUBLANES, F), F32),
                   jax.ShapeDtypeStruct((B, V7X_SUBLANES, F), F32),
                   jax.ShapeDtypeStruct((R, F), F32),
                   jax.ShapeDtypeStruct((R, F), F32)],
        scratch_shapes=[pltpu.VMEM((2, K, tn), BF16),
                        pltpu.VMEM((2, V7X_SUBLANES, tn), F32)],
        compiler_params=_cparams(2), name="ffn_up")(h, hs, w_up, w_up, cw, cw, cb, cb)


def _zero_cols_body(a_ref, o_ref):
    del a_ref
    o_ref[...] = jnp.zeros_like(o_ref)


def _zero_cols(a, col0, tm):
    M, N = a.shape
    tn = N - col0
    assert col0 % tn == 0
    return pl.pallas_call(
        _zero_cols_body, grid=(M // tm,),
        in_specs=[pl.BlockSpec(memory_space=pl.ANY)],
        out_specs=pl.BlockSpec((tm, tn), lambda m: (m, col0 // tn)),
        out_shape=jax.ShapeDtypeStruct((M, N), a.dtype),
        input_output_aliases={0: 0},
        compiler_params=_cparams(1), name="zero_cols")(a)


def _ffn_down_body(a_ref, as_ref, w_ref, o_ref, os_ref, wb_ref, acc_ref, accs_ref,
                   *, F, tk, chunk):
    n = pl.program_id(0)
    k = pl.program_id(1)
    m = pl.program_id(2)

    @pl.when(jnp.logical_and(jnp.logical_and(n == 0, k == 0), m == 0))
    def _():
        acc_ref[...] = jnp.zeros_like(acc_ref)
        accs_ref[...] = jnp.zeros_like(accs_ref)

    @pl.when(m == 0)
    def _():
        _cast_rows(lambda rows: w_ref[rows, :], wb_ref, tk, chunk, row0=k * tk, valid_rows=F)
        ts = (jnp.dot(as_ref[...], wb_ref[...], preferred_element_type=F32)
              + jnp.where(k == 0, 0.0, accs_ref[...]))
        accs_ref[...] = ts
        os_ref[...] = ts

    t = (jnp.dot(a_ref[...], wb_ref[...], preferred_element_type=F32)
         + jnp.where(k == 0, 0.0, acc_ref[m]))
    acc_ref[m] = t
    o_ref[...] = t


def _ffn_down(act, act_s, w_down, layer, F, tk, tn, tm):
    M, FP = act.shape
    R = act_s.shape[0]
    N = w_down.shape[-1]
    nk = FP // tk
    kmax = (F - 1) // tk
    return pl.pallas_call(
        functools.partial(_ffn_down_body, F=F, tk=tk,
                          chunk=_pick_tile(tk, 512, V7X_BF16_ROWS)),
        grid=(N // tn, nk, M // tm),
        in_specs=[pl.BlockSpec((tm, tk), lambda n, k, m: (m, k)),
                  pl.BlockSpec((R, tk), lambda n, k, m: (0, k)),
                  pl.BlockSpec((None, tk, tn),
                               lambda n, k, m: (layer, jnp.minimum(k, kmax), n))],
        out_specs=[pl.BlockSpec((tm, tn), lambda n, k, m: (jnp.where(k == nk - 1, m, 0), n)),
                   pl.BlockSpec((R, tn), lambda n, k, m: (0, n))],
        out_shape=[jax.ShapeDtypeStruct((M, N), F32), jax.ShapeDtypeStruct((R, N), F32)],
        scratch_shapes=[pltpu.VMEM((tk, tn), BF16),
                        pltpu.VMEM((M // tm, tm, tn), F32),
                        pltpu.VMEM((R, tn), F32)],
        compiler_params=_cparams(3), name="ffn_down")(act, act_s, w_down)


def _whole(shape):
    nd = len(shape)
    return pl.BlockSpec(shape, lambda *_: (0,) * nd)


def _s_ffn_glue_body(g_ref, v_ref, h0_ref, h1_ref, cw_ref, cb_ref, o_ref, *, F, FP):
    w = cw_ref[...]
    b = cb_ref[...]
    h0 = h0_ref[...]
    h1 = h1_ref[...]
    gc = _conv3_step(h0[:, :F], h1[:, :F], g_ref[...], w[:, :F]) + b[:, :F]
    vc = _conv3_step(h0[:, F:], h1[:, F:], v_ref[...], w[:, F:]) + b[:, F:]
    act = (_silu(gc) * vc).astype(o_ref.dtype)
    o_ref[:, :F] = act
    if FP > F:
        o_ref[:, F:] = jnp.zeros((act.shape[0], FP - F), o_ref.dtype)


def _s_ffn_glue(g, v, h0, h1, cw, cb, F, FP):
    R = g.shape[0]
    args = (g, v, h0, h1, cw, cb)
    return pl.pallas_call(
        functools.partial(_s_ffn_glue_body, F=F, FP=FP), grid=(1,),
        in_specs=[_whole(a.shape) for a in args],
        out_specs=_whole((R, FP)),
        out_shape=jax.ShapeDtypeStruct((R, FP), BF16),
        compiler_params=_cparams(1), name="sample_ffn_glue")(*args)


def _s_pool_body(x_ref, st_ref, gpre_ref, gpost_ref, gnext_ref, sc_ref, w_ref,
                 xo_ref, ho_ref, hs_ref, *, PG, NH, past):
    x = x_ref[...]
    h = _rms(x, gpre_ref[...])
    hs_ref[...] = h
    outs = []
    for g, w in enumerate(POOL_WINDOWS):
        cols = slice(g * PG, (g + 1) * PG)
        hg = h[:, cols]
        a = hg
        for i in range(1, w):
            a = a + st_ref[NH - i, :, cols]
        pooled = a / float(min(past + 1, w)) - hg
        outs.append(jnp.dot(pooled.astype(BF16), w_ref[g], preferred_element_type=F32))
    mixed = jnp.concatenate(outs, axis=1) * sc_ref[...]
    xn = x + _rms(mixed, gpost_ref[...])
    xo_ref[...] = xn
    ho_ref[...] = _rms(xn, gnext_ref[...]).astype(BF16)


def _s_pool(x, st, g_pre, g_post, g_next, scale, wpool_bf16, past):
    R, D = x.shape
    NH = st.shape[0]
    PG = wpool_bf16.shape[1]
    args = (x, st, g_pre.reshape(1, D), g_post.reshape(1, D), g_next.reshape(1, D),
            scale.reshape(1, D), wpool_bf16)
    return pl.pallas_call(
        functools.partial(_s_pool_body, PG=PG, NH=NH, past=past), grid=(1,),
        in_specs=[_whole(a.shape) for a in args],
        out_specs=[_whole((R, D)), _whole((R, D)), _whole((R, D))],
        out_shape=[jax.ShapeDtypeStruct((R, D), F32), jax.ShapeDtypeStruct((R, D), BF16),
                   jax.ShapeDtypeStruct((R, D), F32)],
        compiler_params=_cparams(1), name="sample_pool")(*args)


def _s_attn_body(pt_ref, q_ref, kn_ref, vn_ref, lfn_ref, dmask_ref, *refs,
                 H, Dh, PAGE, PPS, scale):
    ck_refs, cv_refs, clf_refs = refs[:PPS], refs[PPS:2 * PPS], refs[2 * PPS:3 * PPS]
    o_ref, m_ref, l_ref, acc_ref, run_ref = refs[3 * PPS:]
    p = pl.program_id(1)
    n_steps = pl.num_programs(1)
    PH = PAGE * H
    q = q_ref[...]
    qb = q.astype(BF16)

    @pl.when(p == 0)
    def _():
        m_ref[...] = jnp.full(m_ref.shape, -jnp.inf, F32)
        l_ref[...] = jnp.zeros_like(l_ref)
        acc_ref[...] = jnp.zeros_like(acc_ref)
        run_ref[...] = lfn_ref[...]

    run = run_ref[...]
    scores = []
    for i in range(PPS):
        lf = clf_refs[i][...]
        lane = lax.broadcasted_iota(jnp.int32, lf.shape, 1)
        suf = lf
        tot = lf
        s = H
        while s < PH:
            suf = suf + jnp.where(lane + s < PH, pltpu.roll(suf, PH - s, axis=1), 0.0)
            tot = tot + pltpu.roll(tot, s, axis=1)
            s *= 2
        bias = (suf - lf) + run
        run = run + tot
        kb = ck_refs[i][...].reshape(PH, Dh).astype(BF16)
        scores.append(_dot_nt(qb, kb) * scale + bias + dmask_ref[...])
    run_ref[...] = run

    m_old = m_ref[...]
    m_new = m_old
    for sc in scores:
        m_new = jnp.maximum(m_new, jnp.max(sc, axis=-1, keepdims=True))
    alpha = jnp.exp(m_old - m_new)
    l = alpha * l_ref[...]
    acc = alpha * acc_ref[...]
    for i, sc in enumerate(scores):
        pr = jnp.exp(sc - m_new)
        l = l + jnp.sum(pr, axis=-1, keepdims=True)
        vb = cv_refs[i][...].reshape(PH, Dh).astype(BF16)
        acc = acc + jnp.dot(pr.astype(BF16), vb, preferred_element_type=F32)
    l_ref[...] = l
    acc_ref[...] = acc
    m_ref[...] = m_new

    @pl.when(p == n_steps - 1)
    def _():
        sn = jnp.sum(q * kn_ref[...], axis=-1, keepdims=True) * scale
        m_o = m_ref[...]
        m_n = jnp.maximum(m_o, sn)
        a = jnp.exp(m_o - m_n)
        pn = jnp.exp(sn - m_n)
        l = a * l_ref[...] + pn
        acc = a * acc_ref[...] + pn * vn_ref[...]
        o_ref[...] = acc / l


def _s_attn(q, kn, vn, lfn, cache_k, cache_v, cache_lf, layer, page_table):
    DB, H, Dh = q.shape
    PAGE = cache_k.shape[2]
    assert PAGE & (PAGE - 1) == 0
    NP = page_table.shape[1]
    PH = PAGE * H
    n_pool = cache_k.shape[1]
    clf_flat = cache_lf.reshape(cache_lf.shape[0], n_pool, 1, PH)
    lfn_flat = jnp.tile(lfn[:, None, :], (1, 1, PAGE))
    head_of_lane = jnp.arange(PH, dtype=jnp.int32) % H
    dmask = jnp.where(head_of_lane[None, :] == jnp.arange(H, dtype=jnp.int32)[:, None],
                      0.0, -jnp.inf).astype(F32)
    pt_flat = page_table.reshape(-1).astype(jnp.int32)

    PPS = _pick_tile(NP, SAMPLE_PAGES_PER_STEP, 1)

    def page(i):
        return lambda b, p, pt: pt[b * NP + (NP - 1 - (p * PPS + i))]

    def kv_spec(i):
        return pl.BlockSpec((None, None, PAGE, H, Dh),
                            lambda b, p, pt: (layer, page(i)(b, p, pt), 0, 0, 0))

    def lf_spec(i):
        return pl.BlockSpec((None, None, 1, PH),
                            lambda b, p, pt: (layer, page(i)(b, p, pt), 0, 0))

    per_req = pl.BlockSpec((None, H, Dh), lambda b, p, pt: (b, 0, 0))
    grid_spec = pltpu.PrefetchScalarGridSpec(
        num_scalar_prefetch=1, grid=(DB, NP // PPS),
        in_specs=([per_req, per_req, per_req,
                   pl.BlockSpec((None, 1, PH), lambda b, p, pt: (b, 0, 0)),
                   pl.BlockSpec((H, PH), lambda b, p, pt: (0, 0))]
                  + [kv_spec(i) for i in range(PPS)] + [kv_spec(i) for i in range(PPS)]
                  + [lf_spec(i) for i in range(PPS)]),
        out_specs=per_req,
        scratch_shapes=[pltpu.VMEM((H, 1), F32), pltpu.VMEM((H, 1), F32),
                        pltpu.VMEM((H, Dh), F32), pltpu.VMEM((1, PH), F32)])
    return pl.pallas_call(
        functools.partial(_s_attn_body, H=H, Dh=Dh, PAGE=PAGE, PPS=PPS, scale=Dh ** -0.5),
        grid_spec=grid_spec,
        out_shape=jax.ShapeDtypeStruct((DB, H, Dh), F32),
        compiler_params=_cparams(2), name="attn_sample")(
            pt_flat, q, kn, vn, lfn_flat, dmask, *([cache_k] * PPS), *([cache_v] * PPS),
            *([clf_flat] * PPS))


def _pad_rows(a, rows):
    return jnp.pad(a, ((0, rows - a.shape[0]),) + ((0, 0),) * (a.ndim - 1))


def kernel(x_prompt, x_sample, cache_k, cache_v, cache_logf, page_table, state_conv_b,
           state_pool, state_ffn_conv, w_in_mix, b_forget, w_conv_b, w_out_mix, w_pool,
           pool_scale, w_up, w_ffn_conv, b_ffn_conv, w_down, g_mix_pre, g_mix_post,
           g_ffn_pre, g_ffn_post):
    B, S, D = x_prompt.shape
    DB = x_sample.shape[0]
    assert x_sample.shape[1] == 1
    depth = w_up.shape[0]
    n_attn = w_in_mix.shape[0]
    H, Dh = cache_k.shape[3], cache_k.shape[4]
    WA = H * Dh
    WB = D - WA
    F = w_down.shape[1]
    IN_W = w_in_mix.shape[-1]
    assert IN_W == 3 * WA + H + 3 * WB
    PAGE = cache_k.shape[2]
    past = page_table.shape[1] * PAGE
    NH = state_pool.shape[2]
    M = B * S
    R = SAMPLE_ROWS
    assert DB <= R and H % V7X_SUBLANES == 0 or H < V7X_SUBLANES

    tm = _pick_tile(S, PROMPT_ROW_TILE, V7X_BF16_ROWS)
    tn = 2 * V7X_MXU_DIM
    tn_a = _pick_tile(WA, tn, V7X_LANES)
    tn_b = _pick_tile(WB, V7X_MXU_DIM, V7X_LANES)
    tn_d = _pick_tile(D, tn, V7X_LANES)
    tn_f = _pick_tile(F, V7X_MXU_DIM, V7X_LANES)
    tk_f = _round_up(-(-F // 4), tn_f)
    FP = 4 * tk_f
    tq = _pick_tile(S, ATTN_TILE, V7X_LANES)

    xp = x_prompt.reshape(M, D)
    xs = _pad_rows(x_sample.reshape(DB, D), R)
    wpool_bf16 = w_pool.astype(BF16)
    w_in_t = jnp.swapaxes(w_in_mix, 1, 2)

    hp = _rms_cast(xp, g_mix_pre[0])
    hs = _rms_cast(xs, g_mix_pre[0])

    k_all = v_all = None
    lp_l, ks_l, vs_l, ls_l = [], [], [], []
    cbp_l, cbs_l, plp_l, pls_l, ffp_l, ffs_l = [], [], [], [], [], []
    for layer in range(depth):
        j = layer // 2
        g_next = g_ffn_pre[layer]
        if layer % 2 == 0:
            q, qs = _proj(hp, hs, w_in_t, j, 0, WA, tn_a, tm, BF16, "proj_q")
            k_all, ksn = _proj(hp, hs, w_in_t, j, WA, WA, tn_a, tm, F32, "proj_k",
                               stack=n_attn, prev=k_all)
            v_all, vsn = _proj(hp, hs, w_in_t, j, 2 * WA, WA, tn_a, tm, F32, "proj_v",
                               stack=n_attn, prev=v_all)
            lf, lfs = _logf(hp, hs, w_in_t, j, 3 * WA, b_forget[j], tm)
            c = _cumsum_lanes(lf.reshape(B, S, H).transpose(0, 2, 1))
            att = _attn_prompt(q, (k_all, j), (v_all, j), c, B, H, Dh, tq)
            st = _pad_rows(state_conv_b[j], R)
            cg, ub, cgs, us = _convb(hp, hs, w_in_t, j, 3 * WA + H, w_conv_b[j],
                                     st[:, 0], st[:, 1], B, tn_b, tm)
            qs = qs[:DB].reshape(DB, H, Dh)
            ksn = ksn[:DB].reshape(DB, H, Dh)
            vsn = vsn[:DB].reshape(DB, H, Dh)
            att_s = _s_attn(qs, ksn, vsn, lfs[:DB], cache_k, cache_v, cache_logf, j,
                            page_table)
            att_s = _pad_rows(att_s.reshape(DB, WA), R).astype(BF16)
            fp, fs = _out_proj(att, cg, att_s, cgs, w_out_mix, j, tn_d, tm)
            lp_l.append(lf.reshape(B, S, H))
            cbp_l.append(ub[:, V7X_SUBLANES - (CONV_W - 1):])
            ks_l.append(ksn[:, None])
            vs_l.append(vsn[:, None])
            ls_l.append(lfs[:DB, None])
            cbs_l.append(jnp.stack([state_conv_b[j][:, 1], us[:DB]], axis=1))
            xp, hp = _resnorm(xp, fp, g_mix_post[layer], g_next)
            xs, hs = _resnorm(xs, fs, g_mix_post[layer], g_next)
        else:
            xp, hp, ph = _pool_prompt(xp, B, g_mix_pre[layer], g_mix_post[layer], g_next,
                                      pool_scale[j], wpool_bf16[j])
            plp_l.append(ph[:, 2 * V7X_SUBLANES - NH:])
            st = _pad_rows(state_pool[j], R).transpose(1, 0, 2)
            xs, hs, hs_f32 = _s_pool(xs, st, g_mix_pre[layer], g_mix_post[layer], g_next,
                                     pool_scale[j], wpool_bf16[j], past)
            pls_l.append(jnp.concatenate([state_pool[j][:, 1:], hs_f32[:DB, None]], axis=1))

        cw = w_ffn_conv[layer]
        cb = b_ffn_conv[layer].reshape(1, 2 * F)
        act, hg, hv, sg, sv = _ffn_up(hp, hs, w_up, layer, cw, cb, B, F, FP, tn_f, tm)
        if FP > F:
            act = _zero_cols(act, F, tm)
        sf = _pad_rows(state_ffn_conv[layer], R)
        act_s = _s_ffn_glue(sg, sv, sf[:, 0], sf[:, 1], cw, cb, F, FP)
        fp, fs = _ffn_down(act, act_s, w_down, layer, F, tk_f, tn_d, tm)
        ffp_l.append(jnp.concatenate([hg[:, V7X_SUBLANES - 2:], hv[:, V7X_SUBLANES - 2:]],
                                     axis=-1))
        ffs_l.append(jnp.stack([state_ffn_conv[layer][:, 1],
                                jnp.concatenate([sg[:DB], sv[:DB]], axis=-1)], axis=1))

        if layer + 1 < depth and (layer + 1) % 2 == 0:
            g_after = g_mix_pre[layer + 1]
        else:
            g_after = None
        xp, hp = _resnorm(xp, fp, g_ffn_post[layer], g_after)
        xs, hs = _resnorm(xs, fs, g_ffn_post[layer], g_after)

    return (xp.reshape(B, S, D), xs[:DB].reshape(DB, 1, D),
            k_all.reshape(n_attn, B, S, H, Dh), v_all.reshape(n_attn, B, S, H, Dh),
            jnp.stack(lp_l),
            jnp.stack(ks_l), jnp.stack(vs_l), jnp.stack(ls_l),
            jnp.stack(cbp_l), jnp.stack(cbs_l),
            jnp.stack(plp_l), jnp.stack(pls_l),
            jnp.stack(ffp_l), jnp.stack(ffs_l))
```
